```python
import jax, jax.numpy as jnp
from jax import lax
import numpy as np

D_MODEL = 1024
BATCH = 8
SEQ = 8192
DEPTH = 1
DEC_BATCH = 128
DEC_SEQ = 1
PAST_LEN = 8192
PAGE_SIZE = 128

ATT_GROUPS = ((128, 1), (512, 4), (2048, 16))
N_ATT_GROUPS = 3
ATT_HEADS = 8
ATT_HEAD_DIM = 64
ATT_W = ATT_HEADS * ATT_HEAD_DIM
ATT_BLOCK = 128
HG_DK = 128
HG_HEADS = D_MODEL // HG_DK
HG_DV = D_MODEL // HG_HEADS
HG_W = HG_HEADS * HG_DK
HG_VW = HG_HEADS * HG_DV
HG_CHUNK = 64
NORM_EPS = 1e-6
NEG_INF = -1e30
OFF_ATT_GATE = N_ATT_GROUPS * 3 * ATT_W
OFF_HG_Q = OFF_ATT_GATE + ATT_W
OFF_HG_F = OFF_HG_Q + HG_W
OFF_HG_I = OFF_HG_F + HG_W
OFF_HG_GATE = OFF_HG_I + HG_VW
OFF_MERGE_A = OFF_HG_GATE + HG_VW
OFF_MERGE_B = OFF_MERGE_A + D_MODEL
IN_W = OFF_MERGE_B + D_MODEL

kernel_name = 'hybrid_dilated_attn_hgrn2_step'


def rms_norm(x, w):
    xf = x.astype(jnp.float32)
    y = xf * lax.rsqrt(jnp.mean(xf * xf, axis=-1, keepdims=True) + NORM_EPS)
    return y * w.astype(jnp.float32)


def alibi_slopes():
    n = N_ATT_GROUPS * ATT_HEADS
    s = 2.0 ** (-8.0 * np.arange(1, n + 1) / n)
    return s.astype(np.float32).reshape(N_ATT_GROUPS, ATT_HEADS)


def cols(h, w, lo, width):
    return jnp.einsum('ntd,de->nte', h, w[:, lo:lo + width])


def dilated_attn_prompt(q, k, v, window, dil, slopes):
    n_seq, t_len, n_h, e = q.shape
    n_back = window // dil
    blk = ATT_BLOCK
    span = dil * blk
    t_pad = -(-t_len // span) * span
    nb = t_pad // span

    def to_blocks(a):
        a = jnp.pad(a, ((0, 0), (0, t_pad - t_len), (0, 0), (0, 0)))
        return a.reshape(n_seq, nb, blk, dil, n_h, e)

    def with_prev(a):
        prev = jnp.pad(a[:, :-1], ((0, 0), (1, 0), (0, 0), (0, 0), (0, 0), (0, 0)))
        return jnp.concatenate([prev, a], axis=2)

    qb = to_blocks(q)
    kc = with_prev(to_blocks(k))
    vc = with_prev(to_blocks(v))
    steps = np.arange(blk)[:, None] + blk - np.arange(2 * blk)[None, :]
    band = (steps >= 0) & (steps <= n_back)
    exists = (np.arange(nb)[:, None] * blk + np.arange(2 * blk)[None, :] - blk) >= 0
    mask = (band[None] & exists[:, None])[None, :, None, None]
    bias = jnp.asarray(-slopes[:, None, None] * (steps * dil).astype(np.float32))
    s = jnp.einsum('nbqrhe,nbkrhe->nbrhqk', qb, kc).astype(jnp.float32) * (ATT_HEAD_DIM ** -0.5) + bias
    s = jnp.where(mask, s, NEG_INF)
    mx = jnp.max(s, axis=-1, keepdims=True)
    p = jnp.exp(s - mx)
    den = jnp.sum(p, axis=-1)
    lse = mx[..., 0] + jnp.log(den)
    o = jnp.einsum('nbrhqk,nbkrhe->nbrhqe', p, vc.astype(jnp.float32)) / den[..., None]
    o = o.transpose(0, 1, 4, 2, 3, 5).reshape(n_seq, t_pad, n_h, e)[:, :t_len]
    lse = lse.transpose(0, 1, 4, 2, 3).reshape(n_seq, t_pad, n_h)[:, :t_len]
    return o, lse


def dilated_attn_sample(q, k, v, kv_cache, window, dil, slopes):
    n_seq, s_len, n_h, e = q.shape
    rows = kv_cache.shape[1]
    n_back = window // dil
    keys = jnp.concatenate([kv_cache[:, :, 0], k], axis=1)
    vals = jnp.concatenate([kv_cache[:, :, 1], v], axis=1)
    steps = np.arange(n_back + 1)
    idx = (rows + np.arange(s_len))[:, None] - dil * steps[None, :]
    valid = idx >= 0
    idx = np.maximum(idx, 0)
    kg = keys[:, idx]
    vg = vals[:, idx]
    bias = jnp.asarray(-slopes[:, None] * (steps * dil).astype(np.float32)[None, :])
    s = jnp.einsum('nshe,nsjhe->nshj', q, kg).astype(jnp.float32) * (ATT_HEAD_DIM ** -0.5) + bias
    s = jnp.where(valid[None, :, None, :], s, NEG_INF)
    mx = jnp.max(s, axis=-1, keepdims=True)
    p = jnp.exp(s - mx)
    den = jnp.sum(p, axis=-1)
    lse = mx[..., 0] + jnp.log(den)
    o = jnp.einsum('nshj,nsjhe->nshe', p, vg.astype(jnp.float32)) / den[..., None]
    return o, lse


def hgrn2_chunked(q, k, v, logf, s0):
    n_seq, t_len, n_h, dk = q.shape
    dv = v.shape[-1]
    c = min(HG_CHUNK, t_len)
    t_pad = -(-t_len // c) * c
    nc = t_pad // c
    pad = ((0, 0), (0, t_pad - t_len), (0, 0), (0, 0))

    def chunks(a):
        a = jnp.pad(a, pad)
        return jnp.moveaxis(a.reshape(n_seq, nc, c, n_h, a.shape[-1]), 1, 0)

    causal = jnp.asarray(np.tril(np.ones((c, c), dtype=bool)))[None, :, :, None, None]

    def step(state, inp):
        qc, kc, vc, gc = inp
        b = jnp.cumsum(gc, axis=1)
        o_inter = jnp.einsum('nthk,nhkv->nthv', qc * jnp.exp(b), state)
        diff = b[:, :, None] - b[:, None, :]
        decay = jnp.where(causal, jnp.exp(jnp.minimum(diff, 0.0)), 0.0)
        a = jnp.einsum('nthk,nshk,ntshk->ntsh', qc, kc, decay)
        o_intra = jnp.einsum('ntsh,nshv->nthv', a, vc)
        b_end = b[:, -1]
        new_state = jnp.exp(b_end)[..., None] * state + jnp.einsum(
            'nshk,nshv->nhkv', kc * jnp.exp(b_end[:, None] - b), vc)
        return new_state, o_inter + o_intra

    s_fin, o = lax.scan(step, s0, (chunks(q), chunks(k), chunks(v), chunks(logf)))
    o = jnp.moveaxis(o, 0, 1).reshape(n_seq, t_pad, n_h, dv)[:, :t_len]
    return o, s_fin


def mixer_layer(x, kv_caches, hg_state, norm_w, w_in, w_att_proj, w_hg_proj, w_out, hg_norm_w, lb):
    n_seq, t_len, _ = x.shape
    dt = x.dtype
    f32 = jnp.float32
    h = rms_norm(x, norm_w).astype(dt)
    slopes = alibi_slopes()
    outs, lses, new_kv = [], [], []
    for g, (win, dil) in enumerate(ATT_GROUPS):
        base = g * 3 * ATT_W
        q, k, v = [cols(h, w_in, base + j * ATT_W, ATT_W).reshape(n_seq, t_len, ATT_HEADS, ATT_HEAD_DIM)
                   for j in range(3)]
        if kv_caches is None:
            o, lse = dilated_attn_prompt(q, k, v, win, dil, slopes[g])
            keep = min(win, t_len)
            new_kv.append(jnp.stack([k[:, t_len - keep:], v[:, t_len - keep:]], axis=2))
        else:
            o, lse = dilated_attn_sample(q, k, v, kv_caches[g], win, dil, slopes[g])
            new_kv.append(jnp.stack([k, v], axis=2))
        outs.append(o)
        lses.append(lse)
    mix = jax.nn.softmax(jnp.stack(lses, axis=0), axis=0)
    att = jnp.einsum('gnth,gnthe->nthe', mix, jnp.stack(outs, axis=0)).reshape(n_seq, t_len, ATT_W)
    att = att * jax.nn.silu(cols(h, w_in, OFF_ATT_GATE, ATT_W).astype(f32))
    hq = jax.nn.silu(cols(h, w_in, OFF_HG_Q, HG_W).astype(f32)).reshape(
        n_seq, t_len, HG_HEADS, HG_DK) * (HG_DK ** -0.5)
    fl = cols(h, w_in, OFF_HG_F, HG_W).astype(f32).reshape(n_seq, t_len, HG_HEADS, HG_DK)
    lbh = lb.reshape(HG_HEADS, HG_DK)
    logf = jnp.log(lbh + (1.0 - lbh) * jax.nn.sigmoid(fl))
    hk = (1.0 - lbh) * jax.nn.sigmoid(-fl)
    hi = cols(h, w_in, OFF_HG_I, HG_VW).astype(f32).reshape(n_seq, t_len, HG_HEADS, HG_DV)
    if hg_state is None:
        s0 = jnp.zeros((n_seq, HG_HEADS, HG_DK, HG_DV), f32)
    else:
        s0 = hg_state.astype(f32)
    ho, s_new = hgrn2_chunked(hq, hk, hi, logf, s0)
    ho = rms_norm(ho, hg_norm_w).reshape(n_seq, t_len, HG_VW)
    ho = ho * jax.nn.silu(cols(h, w_in, OFF_HG_GATE, HG_VW).astype(f32))
    ga = jax.nn.sigmoid(cols(h, w_in, OFF_MERGE_A, D_MODEL).astype(f32))
    gb = jax.nn.sigmoid(cols(h, w_in, OFF_MERGE_B, D_MODEL).astype(f32))
    merged = ga * jnp.einsum('nta,ad->ntd', att, w_att_proj.astype(f32)) + \
        gb * jnp.einsum('ntv,vd->ntd', ho, w_hg_proj.astype(f32))
    y = jnp.einsum('ntd,de->nte', merged.astype(dt), w_out)
    return x + y.astype(dt), new_kv, s_new.astype(dt)


def setup_inputs(seed: int = 0) -> dict:
    key = jax.random.key(seed)
    ks = jax.random.split(key, 16)
    f32 = jnp.float32

    def nrm(k, shape, scale=1.0):
        return jax.random.normal(k, shape, f32) * scale

    rows = [min(w, PAST_LEN) for w, _ in ATT_GROUPS]
    kv_shape = lambda r: (DEPTH, DEC_BATCH, r, 2, ATT_HEADS, ATT_HEAD_DIM)
    return {
        'x_prompt': nrm(ks[0], (BATCH, SEQ, D_MODEL)),
        'x_sample': nrm(ks[1], (DEC_BATCH, DEC_SEQ, D_MODEL)),
        'cache_kv_w128': nrm(ks[2], kv_shape(rows[0])),
        'cache_kv_w512': nrm(ks[3], kv_shape(rows[1])),
        'cache_kv_w2048': nrm(ks[4], kv_shape(rows[2])),
        'state_hgrn': nrm(ks[5], (DEPTH, DEC_BATCH, HG_HEADS, HG_DK, HG_DV), 0.5),
        'norm_w': 1.0 + nrm(ks[6], (DEPTH, D_MODEL), 0.02),
        'w_in': nrm(ks[7], (DEPTH, D_MODEL, IN_W), D_MODEL ** -0.5),
        'w_att_proj': nrm(ks[8], (DEPTH, ATT_W, D_MODEL), ATT_W ** -0.5),
        'w_hg_proj': nrm(ks[9], (DEPTH, HG_VW, D_MODEL), HG_VW ** -0.5),
        'w_out': nrm(ks[10], (DEPTH, D_MODEL, D_MODEL), D_MODEL ** -0.5),
        'hg_norm_w': 1.0 + nrm(ks[11], (DEPTH, HG_DV), 0.02),
        'hg_lb_logits': nrm(ks[12], (DEPTH + 1, HG_W), 0.1),
        'final_norm_w': 1.0 + nrm(ks[13], (D_MODEL,), 0.02),
    }


def reference(x_prompt, x_sample, cache_kv_w128, cache_kv_w512, cache_kv_w2048, state_hgrn,
              norm_w, w_in, w_att_proj, w_hg_proj, w_out, hg_norm_w, hg_lb_logits, final_norm_w):
    lb_all = jnp.cumsum(jax.nn.softmax(hg_lb_logits.astype(jnp.float32), axis=0), axis=0)
    xp, xs = x_prompt, x_sample
    kv_p = [[], [], []]
    kv_s = [[], [], []]
    st_p, st_s = [], []
    for layer in range(DEPTH):
        wts = (norm_w[layer], w_in[layer], w_att_proj[layer], w_hg_proj[layer], w_out[layer],
               hg_norm_w[layer], lb_all[layer])
        xp, new_kv_p, sp = mixer_layer(xp, None, None, *wts)
        caches = (cache_kv_w128[layer], cache_kv_w512[layer], cache_kv_w2048[layer])
        xs, new_kv_s, ss = mixer_layer(xs, caches, state_hgrn[layer], *wts)
        for g in range(N_ATT_GROUPS):
            kv_p[g].append(new_kv_p[g])
            kv_s[g].append(new_kv_s[g])
        st_p.append(sp)
        st_s.append(ss)
    y_prompt = rms_norm(xp, final_norm_w).astype(x_prompt.dtype)
    y_sample = rms_norm(xs, final_norm_w).astype(x_sample.dtype)
    kv_w128_prompt = jnp.stack(kv_p[0], axis=0)
    kv_w512_prompt = jnp.stack(kv_p[1], axis=0)
    kv_w2048_prompt = jnp.stack(kv_p[2], axis=0)
    state_hgrn_prompt = jnp.stack(st_p, axis=0)
    kv_w128_sample = jnp.stack(kv_s[0], axis=0)
    kv_w512_sample = jnp.stack(kv_s[1], axis=0)
    kv_w2048_sample = jnp.stack(kv_s[2], axis=0)
    state_hgrn_sample = jnp.stack(st_s, axis=0)
    return (y_prompt, y_sample, kv_w128_prompt, kv_w512_prompt, kv_w2048_prompt, state_hgrn_prompt,
            kv_w128_sample, kv_w512_sample, kv_w2048_sample, state_hgrn_sample)
```

```python
import functools

import numpy as np
import jax
import jax.numpy as jnp
from jax import lax
from jax.experimental import pallas as pl
from jax.experimental.pallas import tpu as pltpu

D_MODEL = 1024
ATT_GROUPS = ((128, 1), (512, 4), (2048, 16))
N_ATT_GROUPS = 3
ATT_HEADS = 8
ATT_HEAD_DIM = 64
ATT_W = ATT_HEADS * ATT_HEAD_DIM
ATT_BLOCK = 128
HG_DK = 128
HG_HEADS = D_MODEL // HG_DK
HG_DV = D_MODEL // HG_HEADS
HG_W = HG_HEADS * HG_DK
HG_VW = HG_HEADS * HG_DV
NORM_EPS = 1e-6
NEG_INF = -1e30
OFF_ATT_GATE = N_ATT_GROUPS * 3 * ATT_W
OFF_HG_Q = OFF_ATT_GATE + ATT_W
OFF_HG_F = OFF_HG_Q + HG_W
OFF_HG_I = OFF_HG_F + HG_W
OFF_HG_GATE = OFF_HG_I + HG_VW
OFF_MERGE_A = OFF_HG_GATE + HG_VW
OFF_MERGE_B = OFF_MERGE_A + D_MODEL
IN_W = OFF_MERGE_B + D_MODEL

HG_CHUNK = 128
HG_TIME_BLOCK = 1024
PROJ_ROWS = 1024
PROJ_COLS = 1024
MERGE_ROWS = 256
VMEM_LIMIT = 48 * 1024 * 1024

F32 = jnp.float32
BF16 = jnp.bfloat16


def _alibi_slopes():
    n = N_ATT_GROUPS * ATT_HEADS
    s = 2.0 ** (-8.0 * np.arange(1, n + 1) / n)
    return s.astype(np.float32).reshape(N_ATT_GROUPS, ATT_HEADS)


def _params(n_grid):
    return pltpu.CompilerParams(dimension_semantics=("arbitrary",) * n_grid,
                                vmem_limit_bytes=VMEM_LIMIT)


def _dot_nt(a, b):
    return lax.dot_general(a, b, (((1,), (1,)), ((), ())), preferred_element_type=F32)


def _dot(a, b):
    return jnp.dot(a, b, preferred_element_type=F32)


def _proj_kernel(x_ref, nw_ref, w_ref, o_ref, h_ref):
    @pl.when(pl.program_id(1) == 0)
    def _():
        x = x_ref[...]
        ms = jnp.mean(x * x, axis=-1, keepdims=True)
        h_ref[...] = (x * lax.rsqrt(ms + NORM_EPS) * nw_ref[...]).astype(BF16)

    o_ref[...] = _dot(h_ref[...], w_ref[...])


def _proj_call(x2d, norm_w, w_in_bf16):
    m = x2d.shape[0]
    tm = min(PROJ_ROWS, m)
    assert m % tm == 0 and IN_W % PROJ_COLS == 0
    return pl.pallas_call(
        _proj_kernel,
        grid=(m // tm, IN_W // PROJ_COLS),
        in_specs=[
            pl.BlockSpec((tm, D_MODEL), lambda i, j: (i, 0)),
            pl.BlockSpec((1, D_MODEL), lambda i, j: (0, 0)),
            pl.BlockSpec((D_MODEL, PROJ_COLS), lambda i, j: (0, j)),
        ],
        out_specs=pl.BlockSpec((tm, PROJ_COLS), lambda i, j: (i, j)),
        out_shape=jax.ShapeDtypeStruct((m, IN_W), F32),
        scratch_shapes=[pltpu.VMEM((tm, D_MODEL), BF16)],
        compiler_params=_params(2),
        name="proj",
    )(x2d, norm_w.reshape(1, D_MODEL), w_in_bf16)


def _attn_prompt_kernel(q_ref, k_ref, v_ref, o_ref, lse_ref, kp_ref, vp_ref, *, dil, slopes):
    blk = ATT_BLOCK
    b = pl.program_id(2)

    @pl.when(b == 0)
    def _():
        kp_ref[...] = jnp.zeros_like(kp_ref)
        vp_ref[...] = jnp.zeros_like(vp_ref)

    q = q_ref[0].astype(BF16)
    kc = k_ref[0]
    vc = v_ref[0]
    kall = jnp.concatenate([kp_ref[...], kc], axis=0).astype(BF16)
    vall = jnp.concatenate([vp_ref[...], vc], axis=0).astype(BF16)

    row = lax.broadcasted_iota(jnp.int32, (blk, 2 * blk), 0)
    col = lax.broadcasted_iota(jnp.int32, (blk, 2 * blk), 1)
    steps = row + blk - col
    mask = (steps >= 0) & (steps <= blk) & (col + b * blk >= blk)
    dist = (steps * dil).astype(F32)
    scale = ATT_HEAD_DIM ** -0.5
    for h in range(ATT_HEADS):
        hs = slice(h * ATT_HEAD_DIM, (h + 1) * ATT_HEAD_DIM)
        s = _dot_nt(q[:, hs], kall[:, hs]) * scale + (-float(slopes[h])) * dist
        s = jnp.where(mask, s, NEG_INF)
        mx = jnp.max(s, axis=-1, keepdims=True)
        p = jnp.exp(s - mx)
        den = jnp.sum(p, axis=-1, keepdims=True)
        o = _dot(p.astype(BF16), vall[:, hs]) / den
        o_ref[0, :, hs] = o
        lse_ref[0, :, hs] = jnp.broadcast_to(mx + jnp.log(den), (blk, ATT_HEAD_DIM))

    kp_ref[...] = kc
    vp_ref[...] = vc


def _attn_prompt_call(proj, n_seq, t_len, g):
    win, dil = ATT_GROUPS[g]
    assert win // dil == ATT_BLOCK and t_len % (dil * ATT_BLOCK) == 0
    rows = t_len // dil
    nb = rows // ATT_BLOCK
    cb = IN_W // ATT_W
    proj_v = proj.reshape(n_seq, rows, dil * IN_W)
    spec = lambda j: pl.BlockSpec((1, ATT_BLOCK, ATT_W), lambda n, r, b: (n, b, r * cb + 3 * g + j))
    out_spec = pl.BlockSpec((1, ATT_BLOCK, ATT_W), lambda n, r, b: (n, b, r))
    out_sds = jax.ShapeDtypeStruct((n_seq, rows, dil * ATT_W), F32)
    o, lse = pl.pallas_call(
        functools.partial(_attn_prompt_kernel, dil=dil, slopes=_alibi_slopes()[g]),
        grid=(n_seq, dil, nb),
        in_specs=[spec(0), spec(1), spec(2)],
        out_specs=[out_spec, out_spec],
        out_shape=[out_sds, out_sds],
        scratch_shapes=[pltpu.VMEM((ATT_BLOCK, ATT_W), F32), pltpu.VMEM((ATT_BLOCK, ATT_W), F32)],
        compiler_params=_params(3),
        name=f"attn_prompt_g{g}",
    )(proj_v, proj_v, proj_v)
    return o.reshape(n_seq * t_len, ATT_W), lse.reshape(n_seq * t_len, ATT_W)


def _attn_sample_kernel(p_ref, c0_ref, c1_ref, c2_ref, o_ref, lse_ref, *, slopes):
    nk = ATT_BLOCK
    head_of_lane = lax.broadcasted_iota(jnp.int32, (ATT_HEADS, ATT_W), 1) // ATT_HEAD_DIM
    head_mask = head_of_lane == lax.broadcasted_iota(jnp.int32, (ATT_HEADS, ATT_W), 0)
    back = (nk - lax.broadcasted_iota(jnp.int32, (ATT_HEADS, nk), 1)).astype(F32)
    scale = ATT_HEAD_DIM ** -0.5
    for g, c_ref in enumerate((c0_ref, c1_ref, c2_ref)):
        dil = ATT_GROUPS[g][1]
        base = g * 3 * ATT_W
        q = p_ref[0, :, base:base + ATT_W]
        kn = p_ref[0, :, base + ATT_W:base + 2 * ATT_W].astype(BF16).astype(F32)
        vn = p_ref[0, :, base + 2 * ATT_W:base + 3 * ATT_W].astype(BF16).astype(F32)
        kc = c_ref[0, :, :ATT_W].astype(BF16)
        vc = c_ref[0, :, ATT_W:].astype(BF16)
        qbd = jnp.where(head_mask, jnp.broadcast_to(q, (ATT_HEADS, ATT_W)), 0.0).astype(BF16)
        slope_col = sum(jnp.where(lax.broadcasted_iota(jnp.int32, (ATT_HEADS, 1), 0) == h,
                                  float(slopes[g][h]), 0.0) for h in range(ATT_HEADS))
        s_c = _dot_nt(qbd, kc) * scale - slope_col * (back * float(dil))
        s_n = jnp.sum(qbd.astype(F32) * kn, axis=-1, keepdims=True) * scale
        mx = jnp.maximum(jnp.max(s_c, axis=-1, keepdims=True), s_n)
        p_c = jnp.exp(s_c - mx)
        p_n = jnp.exp(s_n - mx)
        den = jnp.sum(p_c, axis=-1, keepdims=True) + p_n
        o_all = (_dot(p_c.astype(BF16), vc) + p_n.astype(BF16).astype(F32) * vn) / den
        lse = mx + jnp.log(den)
        o_ref[0, g] = jnp.sum(jnp.where(head_mask, o_all, 0.0), axis=0, keepdims=True)
        lse_ref[0, g] = jnp.sum(jnp.where(head_mask, lse, 0.0), axis=0, keepdims=True)


def _attn_sample_call(proj_s, caches):
    n = proj_s.shape[0]
    views = []
    for g, (win, dil) in enumerate(ATT_GROUPS):
        c = caches[g]
        assert c.shape[1] == win and win // dil == ATT_BLOCK, "cache must hold the full window"
        views.append(c.reshape(n, ATT_BLOCK, dil * 2 * ATT_W))
    cspec = pl.BlockSpec((1, ATT_BLOCK, 2 * ATT_W), lambda i: (i, 0, 0))
    ospec = pl.BlockSpec((1, N_ATT_GROUPS, 1, ATT_W), lambda i: (i, 0, 0, 0))
    osds = jax.ShapeDtypeStruct((n, N_ATT_GROUPS, 1, ATT_W), F32)
    o, lse = pl.pallas_call(
        functools.partial(_attn_sample_kernel, slopes=_alibi_slopes()),
        grid=(n,),
        in_specs=[pl.BlockSpec((1, 1, OFF_ATT_GATE), lambda i: (i, 0, 0)), cspec, cspec, cspec],
        out_specs=[ospec, ospec],
        out_shape=[osds, osds],
        compiler_params=_params(1),
        name="attn_sample",
    )(proj_s.reshape(n, 1, IN_W), *views)
    return [o[:, g, 0] for g in range(N_ATT_GROUPS)], [lse[:, g, 0] for g in range(N_ATT_GROUPS)]


def _lower_bound(lbl_ref, layer):
    l = lbl_ref[...].astype(F32)
    e = jnp.exp(l - jnp.max(l, axis=0, keepdims=True))
    return jnp.sum(e[:layer + 1], axis=0, keepdims=True) / jnp.sum(e, axis=0, keepdims=True)


def _hgrn_gates(q_raw, f_raw, lb):
    q = q_raw * jax.nn.sigmoid(q_raw) * (HG_DK ** -0.5)
    sig = jax.nn.sigmoid(f_raw)
    logf = jnp.log(lb + (1.0 - lb) * sig)
    k = (1.0 - lb) * jax.nn.sigmoid(-f_raw)
    return q, k, logf


def _hgrn_level_consts(c):
    t = np.arange(c)[:, None]
    u = np.arange(c)[None, :]
    mats = [u <= t, u > t]
    lv = np.where(t == u, 0, -1).astype(np.int32)
    ms = []
    m = 1
    while m < c:
        e = (t // (2 * m)) * 2 * m + m - 1
        second = (t % (2 * m)) >= m
        mats.append(np.where(second, (u > e) & (u <= t), (u > t) & (u <= e)))
        pair = (t // (2 * m) == u // (2 * m)) & second & ((u % (2 * m)) < m)
        lv = np.where(pair, len(ms) + 1, lv)
        ms.append(m)
        m *= 2
    return np.concatenate(mats, axis=0).astype(np.float32), lv, tuple(ms)


def _hgrn_prompt_kernel(q_ref, f_ref, i_ref, lbl_ref, g_ref, lv_ref, o_ref, st_ref, stt_ref, *,
                        layer, levels, chunk):
    tb = pl.program_id(2)

    @pl.when(tb == 0)
    def _():
        stt_ref[...] = jnp.zeros_like(stt_ref)

    lb = _lower_bound(lbl_ref, layer)
    lv = lv_ref[...]
    row = lax.broadcasted_iota(jnp.int32, (chunk, HG_DK), 0)

    def body(c, carry):
        r0 = pl.multiple_of(c * chunk, chunk)
        q, k, logf = _hgrn_gates(q_ref[0, pl.ds(r0, chunk), :], f_ref[0, pl.ds(r0, chunk), :], lb)
        v = i_ref[0, pl.ds(r0, chunk), :]
        hi = logf.astype(BF16)
        lo = (logf - hi.astype(F32)).astype(BF16)
        e_all = _dot(g_ref[...], hi) + _dot(g_ref[...], lo)
        b = e_all[:chunk]
        rest = e_all[chunk:2 * chunk]
        stt = stt_ref[...]
        o = _dot_nt((q * jnp.exp(b)).astype(BF16), stt.astype(BF16))
        qb = q.astype(BF16)
        kb = k.astype(BF16)
        a = jnp.where(lv == 0, _dot_nt(qb, kb), 0.0)
        for idx, m in enumerate(levels):
            e_m = e_all[(idx + 2) * chunk:(idx + 3) * chunk]
            x = (jnp.where((row & m) != 0, q, k) * jnp.exp(e_m)).astype(BF16)
            a = jnp.where(lv == idx + 1, _dot_nt(x, x), a)
        vb = v.astype(BF16)
        o_ref[0, pl.ds(r0, chunk), :] = o + _dot(a.astype(BF16), vb)
        kd = (k * jnp.exp(rest)).astype(BF16)
        stt_ref[...] = stt * jnp.exp(b[chunk - 1:chunk, :]) + _dot(v.T.astype(BF16), kd)
        return carry

    lax.fori_loop(0, q_ref.shape[1] // chunk, body, 0)

    @pl.when(tb == pl.num_programs(2) - 1)
    def _():
        st_ref[0, 0] = stt_ref[...].T


def _hgrn_prompt_call(proj, lb_logits, n_seq, t_len, layer):
    chunk = HG_CHUNK
    tbk = min(HG_TIME_BLOCK, t_len)
    assert t_len % tbk == 0 and tbk % chunk == 0
    gmat, lv, levels = _hgrn_level_consts(chunk)
    n_slots = lb_logits.shape[0]
    proj_v = proj.reshape(n_seq, t_len, IN_W)
    col = lambda off: pl.BlockSpec((1, tbk, HG_DK), lambda n, h, t: (n, t, off // HG_DK + h))
    o, st = pl.pallas_call(
        functools.partial(_hgrn_prompt_kernel, layer=layer, levels=levels, chunk=chunk),
        grid=(n_seq, HG_HEADS, t_len // tbk),
        in_specs=[
            col(OFF_HG_Q), col(OFF_HG_F), col(OFF_HG_I),
            pl.BlockSpec((n_slots, HG_DK), lambda n, h, t: (0, h)),
            pl.BlockSpec(gmat.shape, lambda n, h, t: (0, 0)),
            pl.BlockSpec(lv.shape, lambda n, h, t: (0, 0)),
        ],
        out_specs=[
            pl.BlockSpec((1, tbk, HG_DV), lambda n, h, t: (n, t, h)),
            pl.BlockSpec((1, 1, HG_DK, HG_DV), lambda n, h, t: (n, h, 0, 0)),
        ],
        out_shape=[
            jax.ShapeDtypeStruct((n_seq, t_len, HG_VW), F32),
            jax.ShapeDtypeStruct((n_seq, HG_HEADS, HG_DK, HG_DV), F32),
        ],
        scratch_shapes=[pltpu.VMEM((HG_DV, HG_DK), F32)],
        compiler_params=_params(3),
        name="hgrn_prompt",
    )(proj_v, proj_v, proj_v, lb_logits, jnp.asarray(gmat, BF16), jnp.asarray(lv))
    return o.reshape(n_seq * t_len, HG_VW), st


def _hgrn_sample_kernel(q_ref, f_ref, i_ref, lbl_ref, s_ref, o_ref, sn_ref, *, layer):
    l = lbl_ref[...].astype(F32)
    e = jnp.exp(l - jnp.max(l, axis=0, keepdims=True))
    lbh = jnp.sum(e[:layer + 1], axis=0) / jnp.sum(e, axis=0)
    q, k, logf = _hgrn_gates(q_ref[0], f_ref[0], lbh)
    f = jnp.exp(logf)
    pad = jnp.zeros((HG_DK - HG_HEADS, HG_DK), F32)
    qt = jnp.concatenate([q, pad], axis=0).T
    kt = jnp.concatenate([k, pad], axis=0).T
    ft = jnp.concatenate([f, pad], axis=0).T
    iv = i_ref[0]
    for h in range(HG_HEADS):
        s_new = ft[:, h:h + 1] * s_ref[0, h] + kt[:, h:h + 1] * iv[h:h + 1, :]
        sn_ref[0, h] = s_new
        o_ref[0, h:h + 1, :] = jnp.sum(qt[:, h:h + 1] * s_new, axis=0, keepdims=True)


def _hgrn_sample_call(proj_s, lb_logits, state, layer):
    n = proj_s.shape[0]
    n_slots = lb_logits.shape[0]
    seg = lambda off, w: proj_s[:, off:off + w].reshape(n, HG_HEADS, w // HG_HEADS)
    vspec = pl.BlockSpec((1, HG_HEADS, HG_DK), lambda i: (i, 0, 0))
    sspec = pl.BlockSpec((1, HG_HEADS, HG_DK, HG_DV), lambda i: (i, 0, 0, 0))
    o, s_new = pl.pallas_call(
        functools.partial(_hgrn_sample_kernel, layer=layer),
        grid=(n,),
        in_specs=[vspec, vspec, vspec,
                  pl.BlockSpec((n_slots, HG_HEADS, HG_DK), lambda i: (0, 0, 0)), sspec],
        out_specs=[vspec, sspec],
        out_shape=[jax.ShapeDtypeStruct((n, HG_HEADS, HG_DV), F32),
                   jax.ShapeDtypeStruct(state.shape, F32)],
        compiler_params=_params(1),
        name="hgrn_sample",
    )(seg(OFF_HG_Q, HG_W), seg(OFF_HG_F, HG_W), seg(OFF_HG_I, HG_VW),
      lb_logits.reshape(n_slots, HG_HEADS, HG_DK), state)
    return o.reshape(n, HG_VW), s_new


def _merge_kernel(o0_ref, o1_ref, o2_ref, l0_ref, l1_ref, l2_ref, ag_ref, ho_ref, hg_ref, ma_ref, mb_ref,
                  x_ref, wa_ref, wh_ref, wo_ref, hnw_ref, fnw_ref, y_ref):
    l0, l1, l2 = l0_ref[...], l1_ref[...], l2_ref[...]
    mx = jnp.maximum(jnp.maximum(l0, l1), l2)
    e0, e1, e2 = jnp.exp(l0 - mx), jnp.exp(l1 - mx), jnp.exp(l2 - mx)
    att = (e0 * o0_ref[...] + e1 * o1_ref[...] + e2 * o2_ref[...]) / (e0 + e1 + e2)
    ag = ag_ref[...]
    att = att * (ag * jax.nn.sigmoid(ag))
    a_br = _dot(att.astype(BF16), wa_ref[...])

    hnw = hnw_ref[...]
    parts = []
    for h in range(HG_HEADS):
        oh = ho_ref[:, h * HG_DV:(h + 1) * HG_DV]
        ms = jnp.mean(oh * oh, axis=-1, keepdims=True)
        parts.append(oh * lax.rsqrt(ms + NORM_EPS) * hnw)
    hg = hg_ref[...]
    ho = jnp.concatenate(parts, axis=-1) * (hg * jax.nn.sigmoid(hg))
    b_br = _dot(ho.astype(BF16), wh_ref[...])

    merged = jax.nn.sigmoid(ma_ref[...]) * a_br + jax.nn.sigmoid(mb_ref[...]) * b_br
    xo = x_ref[...] + _dot(merged.astype(BF16), wo_ref[...])
    ms = jnp.mean(xo * xo, axis=-1, keepdims=True)
    y_ref[...] = xo * lax.rsqrt(ms + NORM_EPS) * fnw_ref[...]


def _merge_call(outs, lses, proj, ho, x2d, w_att, w_hg, w_out, hg_norm_w, final_norm_w):
    m = x2d.shape[0]
    tm = min(MERGE_ROWS, m)
    assert m % tm == 0
    rows = lambda w: pl.BlockSpec((tm, w), lambda i: (i, 0))
    pcol = lambda off, w: pl.BlockSpec((tm, w), lambda i: (i, off // w))
    full = lambda a: pl.BlockSpec(a.shape, lambda i: (0, 0))
    hnw = hg_norm_w.reshape(1, HG_DV)
    fnw = final_norm_w.reshape(1, D_MODEL)
    return pl.pallas_call(
        _merge_kernel,
        grid=(m // tm,),
        in_specs=[rows(ATT_W)] * 6 + [
            pcol(OFF_ATT_GATE, ATT_W), rows(HG_VW), pcol(OFF_HG_GATE, HG_VW),
            pcol(OFF_MERGE_A, D_MODEL), pcol(OFF_MERGE_B, D_MODEL), rows(D_MODEL),
            full(w_att), full(w_hg), full(w_out), full(hnw), full(fnw)],
        out_specs=rows(D_MODEL),
        out_shape=jax.ShapeDtypeStruct((m, D_MODEL), F32),
        compiler_params=_params(1),
        name="merge",
    )(*outs, *lses, proj, ho, proj, proj, proj, x2d, w_att, w_hg, w_out, hnw, fnw)


def kernel(x_prompt, x_sample, cache_kv_w128, cache_kv_w512, cache_kv_w2048, state_hgrn,
           norm_w, w_in, w_att_proj, w_hg_proj, w_out, hg_norm_w, hg_lb_logits, final_norm_w):
    depth = w_in.shape[0]
    assert depth == 1, "the final norm is fused into the single layer's merge kernel"
    layer = 0
    n_seq, t_len, _ = x_prompt.shape
    n_dec, s_len, _ = x_sample.shape
    assert s_len == 1
    w_in_b = w_in[layer].astype(BF16)
    w_att_b = w_att_proj[layer].astype(BF16)
    w_hg_b = w_hg_proj[layer].astype(BF16)
    w_out_b = w_out[layer].astype(BF16)
    caches = (cache_kv_w128[layer], cache_kv_w512[layer], cache_kv_w2048[layer])

    xp = x_prompt.reshape(n_seq * t_len, D_MODEL)
    proj = _proj_call(xp, norm_w[layer], w_in_b)
    outs, lses = zip(*[_attn_prompt_call(proj, n_seq, t_len, g) for g in range(N_ATT_GROUPS)])
    ho, st_p = _hgrn_prompt_call(proj, hg_lb_logits, n_seq, t_len, layer)
    y_prompt = _merge_call(outs, lses, proj, ho, xp, w_att_b, w_hg_b, w_out_b, hg_norm_w[layer],
                           final_norm_w).reshape(x_prompt.shape)
    proj3 = proj.reshape(n_seq, t_len, IN_W)
    kv_p = []
    for g, (win, _) in enumerate(ATT_GROUPS):
        keep = min(win, t_len)
        lo = g * 3 * ATT_W + ATT_W
        kv_p.append(proj3[:, t_len - keep:, lo:lo + 2 * ATT_W].reshape(
            1, n_seq, keep, 2, ATT_HEADS, ATT_HEAD_DIM))

    xs = x_sample.reshape(n_dec, D_MODEL)
    proj_s = _proj_call(xs, norm_w[layer], w_in_b)
    outs_s, lses_s = _attn_sample_call(proj_s, caches)
    ho_s, st_s = _hgrn_sample_call(proj_s, hg_lb_logits, state_hgrn[layer], layer)
    y_sample = _merge_call(outs_s, lses_s, proj_s, ho_s, xs, w_att_b, w_hg_b, w_out_b, hg_norm_w[layer],
                           final_norm_w).reshape(x_sample.shape)
    kv_s = []
    for g in range(N_ATT_GROUPS):
        lo = g * 3 * ATT_W + ATT_W
        kv_s.append(proj_s[:, lo:lo + 2 * ATT_W].reshape(1, n_dec, 1, 2, ATT_HEADS, ATT_HEAD_DIM))

    return (y_prompt, y_sample, kv_p[0], kv_p[1], kv_p[2], st_p[None],
            kv_s[0], kv_s[1], kv_s[2], st_s[None])
```
